```python
import jax, jax.numpy as jnp
from jax import lax
import numpy as np

D_MODEL = 2048
BATCH = 32
SEQ = 256
DEPTH = 2
DEC_BATCH = 2
DEC_SEQ = 4096
PAST_LEN = 512

GRID_W = 64
MLA_HEADS = 8
Q_LORA = 512
KV_LORA = 256
NOPE_DIM = 128
ROPE_DIM = 64
V_DIM = 128
MLA_SCALE = (NOPE_DIM + ROPE_DIM) ** -0.5
NA_HEADS = 8
NA_DIM = 128
NA_SCALE = NA_DIM ** -0.5
WIN_R_MAX = 8
WIN_C = 16
FFN_DIM = 5632
N_EXPERTS = 8
TOP_K = 2
EXPERT_DIM = 7168
ROPE_THETA = 10000.0
EPS = 1e-6
Q_BLOCK = 128
N_MOD = 6
OFF_KV = Q_LORA
OFF_KR = OFF_KV + KV_LORA
OFF_NA = OFF_KR + ROPE_DIM
OFF_GATE = OFF_NA + 3 * NA_HEADS * NA_DIM
IN_COLS = OFF_GATE + 2 * D_MODEL

kernel_name = "hybrid_mla_natten_diffusion_step"


def rmsnorm(x, g):
    xf = x.astype(jnp.float32)
    y = xf * lax.rsqrt(jnp.mean(xf * xf, axis=-1, keepdims=True) + EPS)
    return (y * g.astype(jnp.float32)).astype(x.dtype)


def adaln(cond, w_ada_l, b_ada_l):
    m = jax.nn.silu(cond) @ w_ada_l + b_ada_l
    return jnp.split(m, N_MOD, axis=-1)


def rope_1d(x, pos):
    half = x.shape[-1] // 2
    freqs = ROPE_THETA ** (-jnp.arange(half, dtype=jnp.float32) / half)
    ang = pos.astype(jnp.float32)[:, None] * freqs[None, :]
    cos = jnp.cos(ang)[None, :, None, :].astype(x.dtype)
    sin = jnp.sin(ang)[None, :, None, :].astype(x.dtype)
    x1, x2 = x[..., :half], x[..., half:]
    return jnp.concatenate([x1 * cos - x2 * sin, x1 * sin + x2 * cos], axis=-1)


def axial_rope(x, n_tokens):
    t = jnp.arange(n_tokens)
    rows, cols = t // GRID_W, t % GRID_W
    a = x.shape[-1] // 2
    return jnp.concatenate([rope_1d(x[..., :a], rows), rope_1d(x[..., a:], cols)], axis=-1)


def blocked_attention(q, k, v, scale):
    b, sq, h, dk = q.shape
    nb = sq // Q_BLOCK
    qb = q.reshape(b, nb, Q_BLOCK, h, dk).transpose(1, 0, 2, 3, 4)

    def one_block(q_blk):
        s = jnp.einsum('bqhd,bkhd->bhqk', q_blk, k, preferred_element_type=jnp.float32) * scale
        p = jax.nn.softmax(s, axis=-1).astype(v.dtype)
        return jnp.einsum('bhqk,bkhd->bqhd', p, v)

    o = lax.map(one_block, qb)
    return o.transpose(1, 0, 2, 3, 4).reshape(b, sq, h, v.shape[-1])


def neighbourhood_attention(q, k, v, k_ctx, v_ctx, rpb_l):
    b, n, h, d = q.shape
    rows = n // GRID_W
    win_r = min(WIN_R_MAX, rows)
    kg = k.reshape(b, rows, GRID_W, h, d)
    vg = v.reshape(b, rows, GRID_W, h, d)
    qg = q.reshape(b, rows, GRID_W, h, d).transpose(1, 0, 2, 3, 4)
    col = jnp.arange(GRID_W)
    col_start = jnp.clip(col - WIN_C // 2, 0, GRID_W - WIN_C)
    cidx = col_start[:, None] + jnp.arange(WIN_C)[None, :]
    dc = cidx - col[:, None] + (WIN_C - 1)
    n_win = win_r * WIN_C

    def one_row(args):
        r, q_row = args
        rs = jnp.clip(r - win_r // 2, 0, rows - win_r)
        k_win = lax.dynamic_slice_in_dim(kg, rs, win_r, axis=1)[:, :, cidx]
        v_win = lax.dynamic_slice_in_dim(vg, rs, win_r, axis=1)[:, :, cidx]
        dr = rs + jnp.arange(win_r) - r + (WIN_R_MAX - 1)
        bias = rpb_l[:, dr[None, :, None], dc[:, None, :]]
        s_win = jnp.einsum('bqhd,biqjhd->bhqij', q_row, k_win,
                           preferred_element_type=jnp.float32) * NA_SCALE + bias.astype(jnp.float32)[None]
        s_win = s_win.reshape(b, h, GRID_W, n_win)
        s_ctx = jnp.einsum('bqhd,bkhd->bhqk', q_row, k_ctx,
                           preferred_element_type=jnp.float32) * NA_SCALE
        p = jax.nn.softmax(jnp.concatenate([s_win, s_ctx], axis=-1), axis=-1).astype(v.dtype)
        p_win = p[..., :n_win].reshape(b, h, GRID_W, win_r, WIN_C)
        p_ctx = p[..., n_win:]
        return (jnp.einsum('bhqij,biqjhd->bqhd', p_win, v_win)
                + jnp.einsum('bhqk,bkhd->bqhd', p_ctx, v_ctx))

    o = lax.map(one_row, (jnp.arange(rows), qg))
    return o.transpose(1, 0, 2, 3, 4).reshape(b, n, h, d)


def project_in(h, w_in_l, g_q_l, w_uq_l, g_kv_l):
    b, s, _ = h.shape
    z = h @ w_in_l
    q = jnp.einsum('bsr,rhd->bshd', rmsnorm(z[..., :OFF_KV], g_q_l), w_uq_l)
    ckv = rmsnorm(z[..., OFF_KV:OFF_KR], g_kv_l)
    kr = z[..., OFF_KR:OFF_NA]
    na = z[..., OFF_NA:OFF_GATE].reshape(b, s, 3, NA_HEADS, NA_DIM)
    gates = z[..., OFF_GATE:]
    return q, ckv, kr, na[:, :, 0], na[:, :, 1], na[:, :, 2], gates


def mla_kv(ckv, kr, w_ukv_l):
    kv = jnp.einsum('bsr,rhd->bshd', ckv, w_ukv_l)
    k_nope, v = kv[..., :NOPE_DIM], kv[..., NOPE_DIM:]
    k_rope = jnp.broadcast_to(kr[:, :, None, :], k_nope.shape[:3] + (ROPE_DIM,))
    return jnp.concatenate([k_nope, k_rope], axis=-1), v


def merge_branches(o_mla, o_na, gates, w_oa_l, w_ob_l, w_o_l):
    ya = jnp.einsum('bshd,hdm->bsm', o_mla, w_oa_l)
    yb = jnp.einsum('bshd,hdm->bsm', o_na, w_ob_l)
    ga = jax.nn.sigmoid(gates[..., :D_MODEL])
    gb = jax.nn.sigmoid(gates[..., D_MODEL:])
    return (ga * ya + gb * yb) @ w_o_l


def swiglu(h, w_g, w_u, w_d):
    return (jax.nn.silu(h @ w_g) * (h @ w_u)) @ w_d


def moe_swiglu(h, w_router, w_g, w_u, w_d):
    logits = (h @ w_router).astype(jnp.float32)
    top_v, top_i = lax.top_k(logits, TOP_K)
    wts = jax.nn.softmax(top_v, axis=-1)
    comb = jnp.sum(jax.nn.one_hot(top_i, N_EXPERTS, dtype=jnp.float32) * wts[..., None], axis=-2)
    out = jnp.zeros_like(h)
    for e in range(N_EXPERTS):
        out = out + comb[..., e:e + 1].astype(h.dtype) * swiglu(h, w_g[e], w_u[e], w_d[e])
    return out


def setup_inputs(seed: int = 0) -> dict:
    key = jax.random.key(seed)
    ks = jax.random.split(key, 32)
    n_dense = (DEPTH + 1) // 2
    n_moe = DEPTH // 2

    def nrm(k, shape, s):
        return jax.random.normal(k, shape, jnp.float32) * s

    return {
        "x_prompt": nrm(ks[0], (BATCH, SEQ, D_MODEL), 1.0),
        "x_sample": nrm(ks[1], (DEC_BATCH, DEC_SEQ, D_MODEL), 1.0),
        "cache_mla_ckv": nrm(ks[2], (DEC_BATCH, DEPTH, PAST_LEN, KV_LORA), 1.0),
        "cache_mla_krope": nrm(ks[3], (DEC_BATCH, DEPTH, PAST_LEN, ROPE_DIM), 1.0),
        "cache_na_k": nrm(ks[4], (DEC_BATCH, DEPTH, PAST_LEN, NA_HEADS, NA_DIM), 1.0),
        "cache_na_v": nrm(ks[5], (DEC_BATCH, DEPTH, PAST_LEN, NA_HEADS, NA_DIM), 1.0),
        "c": nrm(ks[6], (DEC_BATCH, D_MODEL), 1.0),
        "c_ctx": nrm(ks[7], (D_MODEL,), 1.0),
        "w_ada": nrm(ks[8], (DEPTH, D_MODEL, N_MOD * D_MODEL), 0.5 * D_MODEL ** -0.5),
        "b_ada": nrm(ks[9], (DEPTH, N_MOD * D_MODEL), 0.02),
        "norm1": 1.0 + nrm(ks[10], (DEPTH, D_MODEL), 0.02),
        "norm2": 1.0 + nrm(ks[11], (DEPTH, D_MODEL), 0.02),
        "w_in": nrm(ks[12], (DEPTH, D_MODEL, IN_COLS), D_MODEL ** -0.5),
        "g_q": 1.0 + nrm(ks[13], (DEPTH, Q_LORA), 0.02),
        "w_uq": nrm(ks[14], (DEPTH, Q_LORA, MLA_HEADS, NOPE_DIM + ROPE_DIM), Q_LORA ** -0.5),
        "g_kv": 1.0 + nrm(ks[15], (DEPTH, KV_LORA), 0.02),
        "w_ukv": nrm(ks[16], (DEPTH, KV_LORA, MLA_HEADS, NOPE_DIM + V_DIM), KV_LORA ** -0.5),
        "rpb": nrm(ks[17], (DEPTH, NA_HEADS, 2 * WIN_R_MAX - 1, 2 * WIN_C - 1), 0.1),
        "w_oa": nrm(ks[18], (DEPTH, MLA_HEADS, V_DIM, D_MODEL), (MLA_HEADS * V_DIM) ** -0.5),
        "w_ob": nrm(ks[19], (DEPTH, NA_HEADS, NA_DIM, D_MODEL), (NA_HEADS * NA_DIM) ** -0.5),
        "w_o": nrm(ks[20], (DEPTH, D_MODEL, D_MODEL), D_MODEL ** -0.5),
        "ffn_w_gate": nrm(ks[21], (n_dense, D_MODEL, FFN_DIM), D_MODEL ** -0.5),
        "ffn_w_up": nrm(ks[22], (n_dense, D_MODEL, FFN_DIM), D_MODEL ** -0.5),
        "ffn_w_down": nrm(ks[23], (n_dense, FFN_DIM, D_MODEL), FFN_DIM ** -0.5),
        "moe_router": nrm(ks[24], (n_moe, D_MODEL, N_EXPERTS), D_MODEL ** -0.5),
        "moe_w_gate": nrm(ks[25], (n_moe, N_EXPERTS, D_MODEL, EXPERT_DIM), D_MODEL ** -0.5),
        "moe_w_up": nrm(ks[26], (n_moe, N_EXPERTS, D_MODEL, EXPERT_DIM), D_MODEL ** -0.5),
        "moe_w_down": nrm(ks[27], (n_moe, N_EXPERTS, EXPERT_DIM, D_MODEL), EXPERT_DIM ** -0.5),
        "final_norm": 1.0 + nrm(ks[28], (D_MODEL,), 0.02),
    }


def reference(x_prompt, x_sample, cache_mla_ckv, cache_mla_krope, cache_na_k, cache_na_v, c, c_ctx,
              w_ada, b_ada, norm1, norm2, w_in, g_q, w_uq, g_kv, w_ukv, rpb, w_oa, w_ob, w_o,
              ffn_w_gate, ffn_w_up, ffn_w_down, moe_router, moe_w_gate, moe_w_up, moe_w_down, final_norm):

    def channel_mixer(l, h):
        i = l // 2
        if l % 2 == 0:
            return swiglu(h, ffn_w_gate[i], ffn_w_up[i], ffn_w_down[i])
        return moe_swiglu(h, moe_router[i], moe_w_gate[i], moe_w_up[i], moe_w_down[i])

    xp = x_prompt
    ckv_list, kr_list, nak_list, nav_list = [], [], [], []
    for l in range(DEPTH):
        sh1, sc1, ga1, sh2, sc2, ga2 = adaln(c_ctx, w_ada[l], b_ada[l])
        h = rmsnorm(xp, norm1[l]) * (1 + sc1) + sh1
        q, ckv, kr, qn, kn, vn, gates = project_in(h, w_in[l], g_q[l], w_uq[l], g_kv[l])
        k_m, v_m = mla_kv(ckv, kr, w_ukv[l])
        o_mla = blocked_attention(q, k_m, v_m, MLA_SCALE)
        o_na = blocked_attention(qn, kn, vn, NA_SCALE)
        xp = xp + ga1 * merge_branches(o_mla, o_na, gates, w_oa[l], w_ob[l], w_o[l])
        h2 = rmsnorm(xp, norm2[l]) * (1 + sc2) + sh2
        xp = xp + ga2 * channel_mixer(l, h2)
        ckv_list.append(ckv)
        kr_list.append(kr)
        nak_list.append(kn)
        nav_list.append(vn)
    y_prompt = rmsnorm(xp, final_norm)

    xs = x_sample
    n_lat = xs.shape[1]
    for l in range(DEPTH):
        sh1, sc1, ga1, sh2, sc2, ga2 = [m[:, None, :] for m in adaln(c, w_ada[l], b_ada[l])]
        h = rmsnorm(xs, norm1[l]) * (1 + sc1) + sh1
        q, ckv, kr, qn, kn, vn, gates = project_in(h, w_in[l], g_q[l], w_uq[l], g_kv[l])
        q = jnp.concatenate([q[..., :NOPE_DIM], axial_rope(q[..., NOPE_DIM:], n_lat)], axis=-1)
        kr_lat = axial_rope(kr[:, :, None, :], n_lat)[:, :, 0, :]
        k_lat, v_lat = mla_kv(ckv, kr_lat, w_ukv[l])
        k_ctx, v_ctx = mla_kv(cache_mla_ckv[:, l], cache_mla_krope[:, l], w_ukv[l])
        o_mla = blocked_attention(q, jnp.concatenate([k_lat, k_ctx], axis=1),
                                  jnp.concatenate([v_lat, v_ctx], axis=1), MLA_SCALE)
        o_na = neighbourhood_attention(qn, kn, vn, cache_na_k[:, l], cache_na_v[:, l], rpb[l])
        xs = xs + ga1 * merge_branches(o_mla, o_na, gates, w_oa[l], w_ob[l], w_o[l])
        h2 = rmsnorm(xs, norm2[l]) * (1 + sc2) + sh2
        xs = xs + ga2 * channel_mixer(l, h2)
    y_sample = rmsnorm(xs, final_norm)

    new_mla_ckv = jnp.stack(ckv_list, axis=1)
    new_mla_krope = jnp.stack(kr_list, axis=1)
    new_na_k = jnp.stack(nak_list, axis=1)
    new_na_v = jnp.stack(nav_list, axis=1)
    return (y_prompt, y_sample, new_mla_ckv, new_mla_krope, new_na_k, new_na_v)
```

```python
import functools

import jax
import jax.numpy as jnp
from jax import lax
from jax.experimental import pallas as pl
from jax.experimental.pallas import tpu as pltpu

D_MODEL = 2048
BATCH = 32
SEQ = 256
DEPTH = 2
DEC_BATCH = 2
DEC_SEQ = 4096
PAST_LEN = 512
GRID_W = 64
GRID_H = DEC_SEQ // GRID_W
HEADS = 8
Q_LORA = 512
KV_LORA = 256
NOPE_DIM = 128
ROPE_DIM = 64
V_DIM = 128
MLA_DK = 256
MLA_SCALE = (NOPE_DIM + ROPE_DIM) ** -0.5
NA_DIM = 128
NA_SCALE = NA_DIM ** -0.5
WIN_R = 8
WIN_C = 16
RPB_R = 2 * WIN_R - 1
RPB_C = 2 * WIN_C - 1
FFN_DIM = 5632
N_EXPERTS = 8
EXPERT_DIM = 7168
ROPE_THETA = 10000.0
EPS = 1e-6
N_MOD = 6
OFF_KV = Q_LORA
OFF_KR = OFF_KV + KV_LORA
OFF_NA = OFF_KR + ROPE_DIM
OFF_GATE = OFF_NA + 3 * HEADS * NA_DIM
LAT_COLS = 896

T_CTX = BATCH * SEQ
T_LAT = DEC_BATCH * DEC_SEQ
T_ALL = T_CTX + T_LAT

LANES = 128
VMEM_LIMIT_BYTES = 56 * 1024 * 1024

TM = 512
TM_MERGE = 256
TM_FFN = 512
TF_FFN = 512
TM_E = 512
TF_E = 1024
N_TILES_E = (2 * T_ALL + N_EXPERTS * (TM_E - 1) + TM_E - 1) // TM_E
P_ROWS = N_TILES_E * TM_E
CH_DISPATCH = 1024
TM_COMBINE = 256
TM_ROUTE = 512
NA_QROWS = 8
NA_KROWS = 16
NA_TQ = NA_QROWS * GRID_W
NA_TK = NA_KROWS * GRID_W
NEG = -1e30

F32 = jnp.float32
BF16 = jnp.bfloat16
SDS = jax.ShapeDtypeStruct


def _cp(*sem):
    return pltpu.CompilerParams(dimension_semantics=sem, vmem_limit_bytes=VMEM_LIMIT_BYTES)


def _resident(shape, index_map):
    return pl.BlockSpec(shape, index_map, pipeline_mode=pl.Buffered(1))


def _rms(x, g):
    return x * lax.rsqrt(jnp.mean(x * x, axis=-1, keepdims=True) + EPS) * g


def _dot(a, b):
    return jnp.dot(a, b, preferred_element_type=F32)


def _dot_nt(a, b):
    return lax.dot_general(a, b, (((1,), (1,)), ((), ())), preferred_element_type=F32)


def _mod_row(tm, rows_per_mod, mod_base):
    return lambda i: (mod_base + (i * tm) // rows_per_mod, 0, 0)


def _adaln_kernel(c_ref, w_ref, b_ref, o_ref):
    a = jax.nn.silu(c_ref[...]).astype(BF16)
    o_ref[0] = _dot(a, w_ref[0].astype(BF16)) + b_ref[0]


def _adaln(cond8, w_ada, b_ada):
    tn = 1024
    return pl.pallas_call(
        _adaln_kernel,
        name="adaln",
        out_shape=SDS((DEPTH, 8, N_MOD * D_MODEL), F32),
        grid=(DEPTH, N_MOD * D_MODEL // tn),
        in_specs=[
            pl.BlockSpec((8, D_MODEL), lambda l, j: (0, 0)),
            pl.BlockSpec((1, D_MODEL, tn), lambda l, j: (l, 0, j)),
            pl.BlockSpec((1, 1, tn), lambda l, j: (l, 0, j)),
        ],
        out_specs=pl.BlockSpec((1, 8, tn), lambda l, j: (l, 0, j)),
        compiler_params=_cp("arbitrary", "arbitrary"),
    )(cond8, w_ada, b_ada.reshape(DEPTH, 1, N_MOD * D_MODEL))


def _rope128(x, cos, sin):
    lane = lax.broadcasted_iota(jnp.int32, x.shape, 1)
    swapped = jnp.where((lane & 31) < 16, pltpu.roll(x, LANES - 16, 1), pltpu.roll(x, 16, 1))
    return x * cos + swapped * sin


def _proj_a_kernel(*refs, rope):
    if rope:
        (x_ref, m_ref, n1_ref, w1_ref, gq_ref, gkv_ref, wq_ref, cos_ref, sin_ref,
         h_ref, q_ref, ckv_ref, kr_ref) = refs
    else:
        (x_ref, m_ref, n1_ref, w1_ref, gq_ref, gkv_ref, wq_ref,
         h_ref, q_ref, ckv_ref, kr_ref) = refs
    h = _rms(x_ref[...], n1_ref[...]) * (1.0 + m_ref[0, 1:2, :]) + m_ref[0, 0:1, :]
    hb = h.astype(BF16)
    h_ref[...] = hb
    z = _dot(hb, w1_ref[...])
    qn = _rms(z[:, :OFF_KV], gq_ref[...]).astype(BF16)
    q = _dot(qn, wq_ref[...])
    ckv_ref[...] = _rms(z[:, OFF_KV:OFF_KR], gkv_ref[...])
    kr = z[:, OFF_KR:LAT_COLS]
    if rope:
        cos = cos_ref[...]
        sin = sin_ref[...]
        kr = _rope128(kr, cos, sin)
    kr_ref[...] = kr[:, :ROPE_DIM]
    for hd in range(HEADS):
        lo = hd * MLA_DK
        q_ref[:, lo:lo + NOPE_DIM] = q[:, lo:lo + NOPE_DIM].astype(BF16)
        qr = q[:, lo + NOPE_DIM:lo + MLA_DK]
        if rope:
            qr = _rope128(qr, cos, sin)
        q_ref[:, lo + NOPE_DIM:lo + MLA_DK] = qr.astype(BF16)


def _proj_a(x, ml, n1, w1, gq, gkv, wq, rope_tabs, rows_per_mod, mod_base):
    t = x.shape[0]
    rope = rope_tabs is not None
    in_specs = [
        pl.BlockSpec((TM, D_MODEL), lambda i: (i, 0)),
        pl.BlockSpec((1, N_MOD, D_MODEL), _mod_row(TM, rows_per_mod, mod_base)),
        _resident((1, D_MODEL), lambda i: (0, 0)),
        _resident((D_MODEL, LAT_COLS), lambda i: (0, 0)),
        _resident((1, Q_LORA), lambda i: (0, 0)),
        _resident((1, KV_LORA), lambda i: (0, 0)),
        _resident((Q_LORA, HEADS * MLA_DK), lambda i: (0, 0)),
    ]
    args = [x, ml, n1, w1, gq, gkv, wq]
    if rope:
        nt = DEC_SEQ // TM
        in_specs += [pl.BlockSpec((TM, LANES), lambda i: (i % nt, 0))] * 2
        args += list(rope_tabs)
    return pl.pallas_call(
        functools.partial(_proj_a_kernel, rope=rope),
        name="proj_latent_rope" if rope else "proj_latent",
        out_shape=(SDS((t, D_MODEL), BF16), SDS((t, HEADS * MLA_DK), BF16),
                   SDS((t, KV_LORA), F32), SDS((t, ROPE_DIM), F32)),
        grid=(t // TM,),
        in_specs=in_specs,
        out_specs=(pl.BlockSpec((TM, D_MODEL), lambda i: (i, 0)),
                   pl.BlockSpec((TM, HEADS * MLA_DK), lambda i: (i, 0)),
                   pl.BlockSpec((TM, KV_LORA), lambda i: (i, 0)),
                   pl.BlockSpec((TM, ROPE_DIM), lambda i: (i, 0))),
        compiler_params=_cp("arbitrary"),
    )(*args)


def _kvup_kernel(ckv_ref, kr_ref, wk_ref, wv_ref, k_ref, v_ref):
    c = ckv_ref[...].astype(BF16)
    kn = _dot(c, wk_ref[...])
    v_ref[...] = _dot(c, wv_ref[...]).astype(BF16)
    kr = kr_ref[...]
    krp = jnp.concatenate([kr, jnp.zeros_like(kr)], axis=-1).astype(BF16)
    for hd in range(HEADS):
        lo = hd * MLA_DK
        k_ref[:, lo:lo + NOPE_DIM] = kn[:, hd * NOPE_DIM:(hd + 1) * NOPE_DIM].astype(BF16)
        k_ref[:, lo + NOPE_DIM:lo + MLA_DK] = krp


def _kvup(ckv, kr, wk, wv):
    t = ckv.shape[0]
    return pl.pallas_call(
        _kvup_kernel,
        name="mla_kv_up",
        out_shape=(SDS((t, HEADS * MLA_DK), BF16), SDS((t, HEADS * V_DIM), BF16)),
        grid=(t // TM,),
        in_specs=[
            pl.BlockSpec((TM, KV_LORA), lambda i: (i, 0)),
            pl.BlockSpec((TM, ROPE_DIM), lambda i: (i, 0)),
            _resident((KV_LORA, HEADS * NOPE_DIM), lambda i: (0, 0)),
            _resident((KV_LORA, HEADS * V_DIM), lambda i: (0, 0)),
        ],
        out_specs=(pl.BlockSpec((TM, HEADS * MLA_DK), lambda i: (i, 0)),
                   pl.BlockSpec((TM, HEADS * V_DIM), lambda i: (i, 0))),
        compiler_params=_cp("arbitrary"),
    )(ckv, kr, wk, wv)


def _na_proj_kernel(*refs, keep_f32):
    if keep_f32:
        h_ref, w_ref, q_ref, k_ref, v_ref, k32_ref, v32_ref = refs
    else:
        h_ref, w_ref, q_ref, k_ref, v_ref = refs
    h = h_ref[...]
    n = HEADS * NA_DIM
    q_ref[...] = _dot(h, w_ref[:, 0:n]).astype(BF16)
    k = _dot(h, w_ref[:, n:2 * n])
    v = _dot(h, w_ref[:, 2 * n:3 * n])
    k_ref[...] = k.astype(BF16)
    v_ref[...] = v.astype(BF16)
    if keep_f32:
        k32_ref[...] = k
        v32_ref[...] = v


def _na_proj(h, wna, keep_f32):
    t = h.shape[0]
    n = HEADS * NA_DIM
    row = lambda i: (i, 0)
    out_shape = [SDS((t, n), BF16)] * 3
    out_specs = [pl.BlockSpec((TM, n), row)] * 3
    if keep_f32:
        out_shape += [SDS((t, n), F32)] * 2
        out_specs += [pl.BlockSpec((TM, n), row)] * 2
    return pl.pallas_call(
        functools.partial(_na_proj_kernel, keep_f32=keep_f32),
        name="na_proj",
        out_shape=tuple(out_shape),
        grid=(t // TM,),
        in_specs=[pl.BlockSpec((TM, D_MODEL), row), _resident((D_MODEL, 3 * n), lambda i: (0, 0))],
        out_specs=tuple(out_specs),
        compiler_params=_cp("arbitrary"),
    )(h, wna)


def _gate_kernel(h_ref, w_ref, ga_ref, gb_ref):
    h = h_ref[...]
    ga_ref[...] = jax.nn.sigmoid(_dot(h, w_ref[:, :D_MODEL])).astype(BF16)
    gb_ref[...] = jax.nn.sigmoid(_dot(h, w_ref[:, D_MODEL:])).astype(BF16)


def _gates(h, wgt):
    t = h.shape[0]
    row = lambda i: (i, 0)
    return pl.pallas_call(
        _gate_kernel,
        name="branch_gates",
        out_shape=(SDS((t, D_MODEL), BF16), SDS((t, D_MODEL), BF16)),
        grid=(t // TM,),
        in_specs=[pl.BlockSpec((TM, D_MODEL), row), _resident((D_MODEL, 2 * D_MODEL), lambda i: (0, 0))],
        out_specs=(pl.BlockSpec((TM, D_MODEL), row), pl.BlockSpec((TM, D_MODEL), row)),
        compiler_params=_cp("arbitrary"),
    )(h, wgt)


def _attn_kernel(*refs, n_src, hb, dk, dv, scale, tk):
    q_ref = refs[0]
    srcs = [(refs[1 + 2 * s], refs[2 + 2 * s]) for s in range(n_src)]
    o_ref = refs[1 + 2 * n_src]
    for hd in range(hb):
        q = q_ref[:, hd * dk:(hd + 1) * dk]
        m = l = acc = None
        for k_ref, v_ref in srcs:
            n_k = k_ref.shape[0]
            for c0 in range(0, n_k, tk):
                c1 = min(c0 + tk, n_k)
                k = k_ref[c0:c1, hd * dk:(hd + 1) * dk]
                v = v_ref[c0:c1, hd * dv:(hd + 1) * dv]
                s = _dot_nt(q, k) * scale
                smax = jnp.max(s, axis=-1, keepdims=True)
                if m is None:
                    m = smax
                    p = jnp.exp(s - m)
                    l = jnp.sum(p, axis=-1, keepdims=True)
                    acc = _dot(p.astype(BF16), v)
                else:
                    m_new = jnp.maximum(m, smax)
                    alpha = jnp.exp(m - m_new)
                    p = jnp.exp(s - m_new)
                    l = alpha * l + jnp.sum(p, axis=-1, keepdims=True)
                    acc = alpha * acc + _dot(p.astype(BF16), v)
                    m = m_new
        o_ref[:, hd * dv:(hd + 1) * dv] = (acc / l).astype(BF16)


def _attention(q, kv_srcs, *, n_batch, sq, dk, dv, scale, hb, tq, name):
    nq = sq // tq
    in_specs = [pl.BlockSpec((tq, hb * dk), lambda b, g, i: (b * nq + i, g))]
    args = [q]
    for k, v in kv_srcs:
        sk = k.shape[0] // n_batch
        in_specs += [pl.BlockSpec((sk, hb * dk), lambda b, g, i: (b, g)),
                     pl.BlockSpec((sk, hb * dv), lambda b, g, i: (b, g))]
        args += [k, v]
    return pl.pallas_call(
        functools.partial(_attn_kernel, n_src=len(kv_srcs), hb=hb, dk=dk, dv=dv, scale=scale, tk=512),
        name=name,
        out_shape=SDS((n_batch * sq, HEADS * dv), BF16),
        grid=(n_batch, HEADS // hb, nq),
        in_specs=in_specs,
        out_specs=pl.BlockSpec((tq, hb * dv), lambda b, g, i: (b * nq + i, g)),
        compiler_params=_cp("arbitrary", "arbitrary", "arbitrary"),
    )(*args)


def _na_window(case, qi):
    if case == 0:
        return max(qi - WIN_R // 2, 0), WIN_R - 1
    if case == 1:
        return qi, WIN_R // 2 - 1
    return min(qi + WIN_R // 2, NA_KROWS - WIN_R), -1


def _na_bias_kernel(rpb_ref, o_ref, tbl_ref):
    l = pl.program_id(0)
    hd = pl.program_id(1)
    shape = (GRID_W, LANES)
    qc = lax.broadcasted_iota(jnp.int32, shape, 0)
    lane = lax.broadcasted_iota(jnp.int32, shape, 1)
    kc = lane & (GRID_W - 1)
    cs = jnp.clip(qc - WIN_C // 2, 0, GRID_W - WIN_C)
    in_win = (kc >= cs) & (kc < cs + WIN_C)
    dcm = kc - qc + (WIN_C - 1)
    neg = jnp.full(shape, NEG, F32)
    for dr in range(RPB_R):
        base = ((l * HEADS + hd) * RPB_R + dr) * RPB_C
        t = neg
        for dc in range(RPB_C):
            t = jnp.where(dcm == dc, rpb_ref[base + dc], t)
        tbl_ref[dr] = jnp.where(in_win, t, neg)
    left = lane < GRID_W
    for case in range(3):
        for qi in range(NA_QROWS):
            k0, dr0 = _na_window(case, qi)
            for j in range(NA_KROWS // 2):
                halves = []
                for ki in (2 * j, 2 * j + 1):
                    if k0 <= ki < k0 + WIN_R:
                        halves.append(tbl_ref[ki - qi + dr0])
                    else:
                        halves.append(neg)
                o_ref[0, case, 0, qi * GRID_W:(qi + 1) * GRID_W, j * LANES:(j + 1) * LANES] = (
                    jnp.where(left, halves[0], halves[1]))


def _na_bias(rpb):
    return pl.pallas_call(
        _na_bias_kernel,
        name="na_bias_tables",
        out_shape=SDS((DEPTH, 3, HEADS, NA_TQ, NA_TK), F32),
        grid=(DEPTH, HEADS),
        in_specs=[pl.BlockSpec(memory_space=pltpu.SMEM)],
        out_specs=pl.BlockSpec((1, 3, 1, NA_TQ, NA_TK), lambda l, h: (l, 0, h, 0, 0)),
        scratch_shapes=[pltpu.VMEM((RPB_R, GRID_W, LANES), F32)],
        compiler_params=_cp("arbitrary", "arbitrary"),
    )(rpb.reshape(-1))


def _na_lat_kernel(q_ref, k_ref, v_ref, kc_ref, vc_ref, b_ref, o_ref):
    blk = pl.program_id(2)
    first = jnp.clip(NA_QROWS * blk - WIN_R // 2, 0, GRID_H - NA_KROWS)
    ks = pl.multiple_of(first * GRID_W, GRID_W)
    q = q_ref[...]
    s_win = _dot_nt(q, k_ref[pl.ds(ks, NA_TK), :]) * NA_SCALE + b_ref[0, 0, 0]
    s_ctx = _dot_nt(q, kc_ref[...]) * NA_SCALE
    m = jnp.maximum(jnp.max(s_win, axis=-1, keepdims=True), jnp.max(s_ctx, axis=-1, keepdims=True))
    p_win = jnp.exp(s_win - m)
    p_ctx = jnp.exp(s_ctx - m)
    l = jnp.sum(p_win, axis=-1, keepdims=True) + jnp.sum(p_ctx, axis=-1, keepdims=True)
    o = _dot(p_win.astype(BF16), v_ref[pl.ds(ks, NA_TK), :]) + _dot(p_ctx.astype(BF16), vc_ref[...])
    o_ref[...] = (o / l).astype(BF16)


def _na_lat(q, k, v, kc, vc, bias, layer):
    nb = GRID_H // NA_QROWS
    qmap = lambda b, h, i: (b * nb + i, h)
    bh = lambda b, h, i: (b, h)
    return pl.pallas_call(
        _na_lat_kernel,
        name="na_latent",
        out_shape=SDS((T_LAT, HEADS * NA_DIM), BF16),
        grid=(DEC_BATCH, HEADS, nb),
        in_specs=[
            pl.BlockSpec((NA_TQ, NA_DIM), qmap),
            pl.BlockSpec((DEC_SEQ, NA_DIM), bh),
            pl.BlockSpec((DEC_SEQ, NA_DIM), bh),
            pl.BlockSpec((PAST_LEN, NA_DIM), bh),
            pl.BlockSpec((PAST_LEN, NA_DIM), bh),
            pl.BlockSpec((1, 1, 1, NA_TQ, NA_TK),
                         lambda b, h, i: (layer, jnp.minimum(i, 1) + i // (nb - 1), h, 0, 0)),
        ],
        out_specs=pl.BlockSpec((NA_TQ, NA_DIM), qmap),
        compiler_params=_cp("arbitrary", "arbitrary", "arbitrary"),
    )(q, k, v, kc, vc, bias)


def _split_bf16(x):
    hi = x.astype(BF16)
    return hi, (x - hi.astype(F32)).astype(BF16)


def _merge_kernel(*refs, moe):
    if moe:
        (om_ref, on_ref, ga_ref, gb_ref, x_ref, m_ref, n2_ref, woa_ref, wob_ref, wo_ref, wr_ref,
         xo_ref, h2_ref, lg_ref) = refs
    else:
        (om_ref, on_ref, ga_ref, gb_ref, x_ref, m_ref, n2_ref, woa_ref, wob_ref, wo_ref,
         xo_ref, h2_ref) = refs
    ya = _dot(om_ref[...], woa_ref[...])
    yb = _dot(on_ref[...], wob_ref[...])
    mix = ga_ref[...].astype(F32) * ya + gb_ref[...].astype(F32) * yb
    x = x_ref[...] + m_ref[0, 2:3, :] * _dot(mix.astype(BF16), wo_ref[...])
    xo_ref[...] = x
    h2 = _rms(x, n2_ref[...]) * (1.0 + m_ref[0, 4:5, :]) + m_ref[0, 3:4, :]
    if moe:
        h2_ref[...] = h2
        h_hi, h_lo = _split_bf16(h2)
        w_hi, w_lo = _split_bf16(wr_ref[...])
        lg_ref[...] = _dot(h_hi, w_hi) + (_dot(h_hi, w_lo) + _dot(h_lo, w_hi))
    else:
        h2_ref[...] = h2.astype(BF16)


def _merge(om, on, ga, gb, x, ml, n2, woa, wob, wo, wr, rows_per_mod, mod_base):
    t = x.shape[0]
    tm = TM_MERGE
    moe = wr is not None
    row = lambda i: (i, 0)
    const = lambda i: (0, 0)
    n = HEADS * V_DIM
    in_specs = [
        pl.BlockSpec((tm, n), row), pl.BlockSpec((tm, n), row),
        pl.BlockSpec((tm, D_MODEL), row), pl.BlockSpec((tm, D_MODEL), row),
        pl.BlockSpec((tm, D_MODEL), row),
        pl.BlockSpec((1, N_MOD, D_MODEL), _mod_row(tm, rows_per_mod, mod_base)),
        _resident((1, D_MODEL), const),
        _resident((n, D_MODEL), const), _resident((n, D_MODEL), const), _resident((D_MODEL, D_MODEL), const),
    ]
    args = [om, on, ga, gb, x, ml, n2, woa, wob, wo]
    out_shape = [SDS((t, D_MODEL), F32), SDS((t, D_MODEL), F32 if moe else BF16)]
    out_specs = [pl.BlockSpec((tm, D_MODEL), row), pl.BlockSpec((tm, D_MODEL), row)]
    if moe:
        in_specs.append(_resident((D_MODEL, LANES), const))
        args.append(wr)
        out_shape.append(SDS((t, LANES), F32))
        out_specs.append(pl.BlockSpec((tm, LANES), row))
    return pl.pallas_call(
        functools.partial(_merge_kernel, moe=moe),
        name="merge_router" if moe else "merge",
        out_shape=tuple(out_shape),
        grid=(t // tm,),
        in_specs=in_specs,
        out_specs=tuple(out_specs),
        compiler_params=_cp("arbitrary"),
    )(*args)


def _ffn_kernel(h_ref, wg_ref, wu_ref, wd_ref, x_ref, m_ref, o_ref, acc_ref):
    j = pl.program_id(1)

    @pl.when(j == 0)
    def _():
        acc_ref[...] = jnp.zeros_like(acc_ref)

    h = h_ref[...]
    a = (jax.nn.silu(_dot(h, wg_ref[...])) * _dot(h, wu_ref[...])).astype(BF16)
    acc_ref[...] += _dot(a, wd_ref[...])

    @pl.when(j == pl.num_programs(1) - 1)
    def _():
        o_ref[...] = x_ref[...] + m_ref[0, 5:6, :] * acc_ref[...]


def _ffn(h2, wg, wu, wd, x, ml, rows_per_mod, mod_base):
    t = x.shape[0]
    tm, tf = TM_FFN, TF_FFN
    row = lambda i, j: (i, 0)
    mod = _mod_row(tm, rows_per_mod, mod_base)
    return pl.pallas_call(
        _ffn_kernel,
        name="ffn_dense",
        out_shape=SDS((t, D_MODEL), F32),
        grid=(t // tm, FFN_DIM // tf),
        in_specs=[
            pl.BlockSpec((tm, D_MODEL), row),
            pl.BlockSpec((D_MODEL, tf), lambda i, j: (0, j)),
            pl.BlockSpec((D_MODEL, tf), lambda i, j: (0, j)),
            pl.BlockSpec((tf, D_MODEL), lambda i, j: (j, 0)),
            pl.BlockSpec((tm, D_MODEL), row),
            pl.BlockSpec((1, N_MOD, D_MODEL), lambda i, j: mod(i)),
        ],
        out_specs=pl.BlockSpec((tm, D_MODEL), row),
        scratch_shapes=[pltpu.VMEM((tm, D_MODEL), F32)],
        compiler_params=_cp("arbitrary", "arbitrary"),
    )(h2, wg, wu, wd, x, ml)


def _route_kernel(lg_ref, ri_ref, rw_ref, cnt_ref, carry_ref):
    tm = lg_ref.shape[0]

    @pl.when(pl.program_id(0) == 0)
    def _():
        carry_ref[...] = jnp.zeros_like(carry_ref)

    lane = lax.broadcasted_iota(jnp.int32, (tm, LANES), 1)
    lane_f = lane.astype(F32)
    lg = jnp.where(lane < N_EXPERTS, lg_ref[...], NEG)
    m1 = jnp.max(lg, axis=-1, keepdims=True)
    i1 = jnp.min(jnp.where(lg == m1, lane_f, float(LANES)), axis=-1, keepdims=True)
    sel1 = lane_f == i1
    lg2 = jnp.where(sel1, NEG, lg)
    m2 = jnp.max(lg2, axis=-1, keepdims=True)
    i2 = jnp.min(jnp.where(lg2 == m2, lane_f, float(LANES)), axis=-1, keepdims=True)
    sel2 = lane_f == i2
    e = jnp.exp(m2 - m1)
    w1 = 1.0 / (1.0 + e)
    w2 = e / (1.0 + e)
    onehot = (sel1 | sel2).astype(F32)
    r = lax.broadcasted_iota(jnp.int32, (tm, tm), 0)
    c = lax.broadcasted_iota(jnp.int32, (tm, tm), 1)
    incl = _dot((r >= c).astype(BF16), onehot.astype(BF16))
    before = incl - onehot + carry_ref[0:1, :]
    r1 = jnp.sum(jnp.where(sel1, before, 0.0), axis=-1, keepdims=True).astype(jnp.int32)
    r2 = jnp.sum(jnp.where(sel2, before, 0.0), axis=-1, keepdims=True).astype(jnp.int32)
    carry_ref[...] = carry_ref[...] + jnp.sum(onehot, axis=0, keepdims=True)
    ri_ref[...] = jnp.where(lane == 0, i1.astype(jnp.int32),
                            jnp.where(lane == 1, i2.astype(jnp.int32),
                                      jnp.where(lane == 2, r1, jnp.where(lane == 3, r2, 0))))
    rw_ref[...] = jnp.where(lane == 0, w1, jnp.where(lane == 1, w2, 0.0))
    cnt_ref[...] = carry_ref[...].astype(jnp.int32)


def _route(logits):
    t = logits.shape[0]
    tm = TM_ROUTE
    row = lambda i: (i, 0)
    return pl.pallas_call(
        _route_kernel,
        name="route_top2",
        out_shape=(SDS((t, LANES), jnp.int32), SDS((t, LANES), F32), SDS((8, LANES), jnp.int32)),
        grid=(t // tm,),
        in_specs=[pl.BlockSpec((tm, LANES), row)],
        out_specs=(pl.BlockSpec((tm, LANES), row), pl.BlockSpec((tm, LANES), row),
                   pl.BlockSpec((8, LANES), lambda i: (0, 0))),
        scratch_shapes=[pltpu.VMEM((8, LANES), F32)],
        compiler_params=_cp("arbitrary"),
    )(logits)


def _row_copy(src, src_row, dst, dst_row, sem):
    return pltpu.make_async_copy(src.at[pl.ds(src_row, 1)], dst.at[pl.ds(dst_row, 1)], sem)


def _dispatch_kernel(starts_ref, rt_ref, h_hbm, xs_in, xs_hbm, sem):
    del xs_in
    base = pl.program_id(0) * CH_DISPATCH

    def issue(r, carry):
        p1 = starts_ref[rt_ref[0, 0, r]] + rt_ref[0, 2, r]
        p2 = starts_ref[rt_ref[0, 1, r]] + rt_ref[0, 3, r]
        _row_copy(h_hbm, base + r, xs_hbm, p1, sem).start()
        _row_copy(h_hbm, base + r, xs_hbm, p2, sem).start()
        return carry

    lax.fori_loop(0, CH_DISPATCH, issue, 0)

    def drain(r, carry):
        _row_copy(h_hbm, 0, xs_hbm, 0, sem).wait()
        _row_copy(h_hbm, 0, xs_hbm, 0, sem).wait()
        return carry

    lax.fori_loop(0, CH_DISPATCH, drain, 0)


def _dispatch(starts, rt, h2, xs):
    t = h2.shape[0]
    return pl.pallas_call(
        _dispatch_kernel,
        name="moe_dispatch",
        out_shape=SDS(xs.shape, xs.dtype),
        grid_spec=pltpu.PrefetchScalarGridSpec(
            num_scalar_prefetch=1,
            grid=(t // CH_DISPATCH,),
            in_specs=[
                pl.BlockSpec((1, 4, CH_DISPATCH), lambda i, s: (i, 0, 0), memory_space=pltpu.SMEM),
                pl.BlockSpec(memory_space=pl.ANY),
                pl.BlockSpec(memory_space=pl.ANY),
            ],
            out_specs=pl.BlockSpec(memory_space=pl.ANY),
            scratch_shapes=[pltpu.SemaphoreType.DMA(())],
        ),
        input_output_aliases={3: 0},
        compiler_params=_cp("arbitrary"),
    )(starts, rt, h2, xs)


def _expert_kernel(te_ref, nu_ref, x_ref, wg_ref, wu_ref, wd_ref, y_ref, xb_ref):
    del te_ref
    i = pl.program_id(0)
    j = pl.program_id(1)
    used = i < nu_ref[0]

    @pl.when(j == 0)
    def _():
        y_ref[...] = jnp.zeros_like(y_ref)
        xb_ref[...] = x_ref[...].astype(BF16)

    @pl.when(used)
    def _():
        xb = xb_ref[...]
        a = (jax.nn.silu(_dot(xb, wg_ref[0])) * _dot(xb, wu_ref[0])).astype(BF16)
        y_ref[...] += _dot(a, wd_ref[0])


def _experts(te, nu, xs, wg, wu, wd):
    nf = EXPERT_DIM // TF_E

    def col(i, j, te, nu):
        return jnp.where(i < nu[0], j, nf - 1)

    return pl.pallas_call(
        _expert_kernel,
        name="moe_experts",
        out_shape=SDS((P_ROWS, D_MODEL), F32),
        grid_spec=pltpu.PrefetchScalarGridSpec(
            num_scalar_prefetch=2,
            grid=(N_TILES_E, nf),
            in_specs=[
                pl.BlockSpec((TM_E, D_MODEL), lambda i, j, te, nu: (jnp.minimum(i, nu[0] - 1), 0)),
                pl.BlockSpec((1, D_MODEL, TF_E), lambda i, j, te, nu: (te[i], 0, col(i, j, te, nu))),
                pl.BlockSpec((1, D_MODEL, TF_E), lambda i, j, te, nu: (te[i], 0, col(i, j, te, nu))),
                pl.BlockSpec((1, TF_E, D_MODEL), lambda i, j, te, nu: (te[i], col(i, j, te, nu), 0)),
            ],
            out_specs=pl.BlockSpec((TM_E, D_MODEL), lambda i, j, te, nu: (i, 0)),
            scratch_shapes=[pltpu.VMEM((TM_E, D_MODEL), BF16)],
        ),
        compiler_params=_cp("arbitrary", "arbitrary"),
    )(te, nu, xs, wg, wu, wd)


def _combine_kernel(starts_ref, rt_ref, rw_ref, x_ref, m_ref, fn_ref, y_hbm, o_ref, yg_ref, sem):
    tm = x_ref.shape[0]

    def issue(r, carry):
        p1 = starts_ref[rt_ref[0, 0, r]] + rt_ref[0, 2, r]
        p2 = starts_ref[rt_ref[0, 1, r]] + rt_ref[0, 3, r]
        _row_copy(y_hbm, p1, yg_ref.at[0], r, sem).start()
        _row_copy(y_hbm, p2, yg_ref.at[1], r, sem).start()
        return carry

    lax.fori_loop(0, tm, issue, 0)

    def drain(r, carry):
        _row_copy(y_hbm, 0, yg_ref.at[0], 0, sem).wait()
        _row_copy(y_hbm, 0, yg_ref.at[1], 0, sem).wait()
        return carry

    lax.fori_loop(0, tm, drain, 0)
    rw = rw_ref[...]
    mixed = rw[:, 0:1] * yg_ref[0] + rw[:, 1:2] * yg_ref[1]
    x = x_ref[...] + m_ref[0, 5:6, :] * mixed
    o_ref[...] = _rms(x, fn_ref[...])


def _combine(starts, rt, rw, x, ml, fnorm, y, rows_per_mod, mod_base):
    t = x.shape[0]
    tm = TM_COMBINE
    mod = _mod_row(tm, rows_per_mod, mod_base)
    return pl.pallas_call(
        _combine_kernel,
        name="moe_combine",
        out_shape=SDS((t, D_MODEL), F32),
        grid_spec=pltpu.PrefetchScalarGridSpec(
            num_scalar_prefetch=1,
            grid=(t // tm,),
            in_specs=[
                pl.BlockSpec((1, 4, tm), lambda i, s: (i, 0, 0), memory_space=pltpu.SMEM),
                pl.BlockSpec((tm, LANES), lambda i, s: (i, 0)),
                pl.BlockSpec((tm, D_MODEL), lambda i, s: (i, 0)),
                pl.BlockSpec((1, N_MOD, D_MODEL), lambda i, s: mod(i)),
                pl.BlockSpec((1, D_MODEL), lambda i, s: (0, 0)),
                pl.BlockSpec(memory_space=pl.ANY),
            ],
            out_specs=pl.BlockSpec((tm, D_MODEL), lambda i, s: (i, 0)),
            scratch_shapes=[pltpu.VMEM((2, tm, D_MODEL), F32), pltpu.SemaphoreType.DMA(())],
        ),
        compiler_params=_cp("arbitrary"),
    )(starts, rt, rw, x, ml, fnorm, y)


def _rope_tables():
    t = jnp.arange(DEC_SEQ)
    half = ROPE_DIM // 4
    freqs = ROPE_THETA ** (-jnp.arange(half, dtype=F32) / half)

    def tab(pos):
        ang = pos.astype(F32)[:, None] * freqs[None, :]
        c, s = jnp.cos(ang), jnp.sin(ang)
        return jnp.concatenate([c, c], axis=-1), jnp.concatenate([-s, s], axis=-1)

    cr, sr = tab(t // GRID_W)
    cc, sc = tab(t % GRID_W)
    cos = jnp.concatenate([cr, cc, cr, cc], axis=-1)
    sin = jnp.concatenate([sr, sc, sr, sc], axis=-1)
    return cos, sin


def _route_tables(ri, t0, t1, ch):
    r = ri[t0:t1, :4].T.reshape(4, (t1 - t0) // ch, ch)
    return jnp.transpose(r, (1, 0, 2))


def _layer_weights(l, w_in, g_q, w_uq, g_kv, w_ukv, w_oa, w_ob, w_o, norm1, norm2):
    wl = w_in[l]
    w1 = jnp.pad(wl[:, :OFF_NA], ((0, 0), (0, LAT_COLS - OFF_NA))).astype(BF16)
    wq = jnp.pad(w_uq[l], ((0, 0), (0, 0), (0, MLA_DK - NOPE_DIM - ROPE_DIM)))
    return dict(
        w1=w1,
        wna=wl[:, OFF_NA:OFF_GATE].astype(BF16),
        wgt=wl[:, OFF_GATE:].astype(BF16),
        wq=wq.reshape(Q_LORA, HEADS * MLA_DK).astype(BF16),
        wk=w_ukv[l][:, :, :NOPE_DIM].reshape(KV_LORA, HEADS * NOPE_DIM).astype(BF16),
        wv=w_ukv[l][:, :, NOPE_DIM:].reshape(KV_LORA, HEADS * V_DIM).astype(BF16),
        woa=w_oa[l].reshape(HEADS * V_DIM, D_MODEL).astype(BF16),
        wob=w_ob[l].reshape(HEADS * NA_DIM, D_MODEL).astype(BF16),
        wo=w_o[l].astype(BF16),
        gq=g_q[l].reshape(1, Q_LORA),
        gkv=g_kv[l].reshape(1, KV_LORA),
        n1=norm1[l].reshape(1, D_MODEL),
        n2=norm2[l].reshape(1, D_MODEL),
    )


def kernel(x_prompt, x_sample, cache_mla_ckv, cache_mla_krope, cache_na_k, cache_na_v, c, c_ctx, w_ada, b_ada, norm1, norm2, w_in, g_q, w_uq, g_kv, w_ukv, rpb, w_oa, w_ob, w_o, ffn_w_gate, ffn_w_up, ffn_w_down, moe_router, moe_w_gate, moe_w_up, moe_w_down, final_norm):
    cond8 = jnp.zeros((8, D_MODEL), F32).at[0].set(c_ctx).at[1:1 + DEC_BATCH].set(c)
    mod = _adaln(cond8, w_ada, b_ada)
    na_bias = _na_bias(rpb)
    rope_tabs = _rope_tables()
    fnorm = final_norm.reshape(1, D_MODEL)

    passes = (
        dict(x=x_prompt.reshape(T_CTX, D_MODEL), rows_per_mod=T_CTX, mod_base=0, latent=False),
        dict(x=x_sample.reshape(T_LAT, D_MODEL), rows_per_mod=DEC_SEQ, mod_base=1, latent=True),
    )
    xs = [p["x"] for p in passes]
    caches = [[], [], [], []]

    for l in range(DEPTH):
        w = _layer_weights(l, w_in, g_q, w_uq, g_kv, w_ukv, w_oa, w_ob, w_o, norm1, norm2)
        ml = mod[l].reshape(8, N_MOD, D_MODEL)
        moe = l % 2 == 1
        wr = jnp.pad(moe_router[l // 2], ((0, 0), (0, LANES - N_EXPERTS))) if moe else None
        mids, h2s, logits = [], [], []
        for p, x in zip(passes, xs):
            rpm, mb = p["rows_per_mod"], p["mod_base"]
            h, q, ckv, kr = _proj_a(x, ml, w["n1"], w["w1"], w["gq"], w["gkv"], w["wq"],
                                    rope_tabs if p["latent"] else None, rpm, mb)
            km, vm = _kvup(ckv, kr, w["wk"], w["wv"])
            na = _na_proj(h, w["wna"], keep_f32=not p["latent"])
            ga, gb = _gates(h, w["wgt"])
            if p["latent"]:
                kc, vc = _kvup(cache_mla_ckv[:, l].reshape(DEC_BATCH * PAST_LEN, KV_LORA),
                               cache_mla_krope[:, l].reshape(DEC_BATCH * PAST_LEN, ROPE_DIM),
                               w["wk"], w["wv"])
                o_mla = _attention(q, [(km, vm), (kc, vc)], n_batch=DEC_BATCH, sq=DEC_SEQ, dk=MLA_DK,
                                   dv=V_DIM, scale=MLA_SCALE, hb=1, tq=512, name="mla_latent")
                nkc = cache_na_k[:, l].reshape(DEC_BATCH * PAST_LEN, HEADS * NA_DIM).astype(BF16)
                nvc = cache_na_v[:, l].reshape(DEC_BATCH * PAST_LEN, HEADS * NA_DIM).astype(BF16)
                o_na = _na_lat(na[0], na[1], na[2], nkc, nvc, na_bias, l)
            else:
                o_mla = _attention(q, [(km, vm)], n_batch=BATCH, sq=SEQ, dk=MLA_DK, dv=V_DIM,
                                   scale=MLA_SCALE, hb=HEADS, tq=SEQ, name="mla_context")
                o_na = _attention(na[0], [(na[1], na[2])], n_batch=BATCH, sq=SEQ, dk=NA_DIM, dv=NA_DIM,
                                  scale=NA_SCALE, hb=HEADS, tq=SEQ, name="na_context")
                caches[0].append(ckv.reshape(BATCH, SEQ, KV_LORA))
                caches[1].append(kr.reshape(BATCH, SEQ, ROPE_DIM))
                caches[2].append(na[3].reshape(BATCH, SEQ, HEADS, NA_DIM))
                caches[3].append(na[4].reshape(BATCH, SEQ, HEADS, NA_DIM))
            res = _merge(o_mla, o_na, ga, gb, x, ml, w["n2"], w["woa"], w["wob"], w["wo"], wr, rpm, mb)
            mids.append(res[0])
            h2s.append(res[1])
            if moe:
                logits.append(res[2])

        if not moe:
            i = l // 2
            wg, wu, wd = (ffn_w_gate[i].astype(BF16), ffn_w_up[i].astype(BF16), ffn_w_down[i].astype(BF16))
            xs = [_ffn(h2, wg, wu, wd, xm, ml, p["rows_per_mod"], p["mod_base"])
                  for p, h2, xm in zip(passes, h2s, mids)]
            continue

        i = l // 2
        ri, rw, cnt = _route(jnp.concatenate(logits, axis=0))
        counts = cnt[0, :N_EXPERTS]
        padded = (counts + (TM_E - 1)) // TM_E * TM_E
        ends = jnp.cumsum(padded)
        starts = (ends - padded).astype(jnp.int32)
        nu = (ends[-1] // TM_E).astype(jnp.int32)
        tile = jnp.arange(N_TILES_E, dtype=jnp.int32)
        te = jnp.minimum(jnp.sum(tile[:, None] * TM_E >= ends[None, :], axis=1), N_EXPERTS - 1).astype(jnp.int32)
        te = jnp.where(tile < nu, te, jnp.max(jnp.where(tile < nu, te, 0)))
        bounds = ((0, T_CTX), (T_CTX, T_ALL))
        xsort = jnp.zeros((P_ROWS, D_MODEL), F32)
        for h2, (t0, t1) in zip(h2s, bounds):
            xsort = _dispatch(starts, _route_tables(ri, t0, t1, CH_DISPATCH), h2, xsort)
        y = _experts(te, nu.reshape(1), xsort, moe_w_gate[i].astype(BF16), moe_w_up[i].astype(BF16),
                     moe_w_down[i].astype(BF16))
        assert l == DEPTH - 1
        xs = [_combine(starts, _route_tables(ri, t0, t1, TM_COMBINE), rw[t0:t1], xm, ml, fnorm, y,
                       p["rows_per_mod"], p["mod_base"])
              for p, xm, (t0, t1) in zip(passes, mids, bounds)]

    y_prompt = xs[0].reshape(BATCH, SEQ, D_MODEL)
    y_sample = xs[1].reshape(DEC_BATCH, DEC_SEQ, D_MODEL)
    return (y_prompt, y_sample) + tuple(jnp.stack(cl, axis=1) for cl in caches)
```
